```python
import math
import jax, jax.numpy as jnp
from jax import lax
import numpy as np

D_MODEL = 1024
BATCH = 32
SEQ = 2048
DEPTH = 1

MOBA_HEADS = 8
MOBA_HEAD_DIM = 64
MOBA_WIDTH = MOBA_HEADS * MOBA_HEAD_DIM
MOBA_ROT_DIM = MOBA_HEAD_DIM // 4
MOBA_BLOCK = 256
MOBA_TOPK = 3
MOBA_Q_CHUNK = 128
MLA_HEADS = 8
MLA_NOPE = 64
MLA_ROPE = 32
MLA_V = 64
MLA_Q_LORA = 384
MLA_KV_LORA = 256
MLA_WIDTH = MLA_HEADS * MLA_V
MLA_Q_BLOCK = 128
ROPE_THETA = 500000.0
D_FF = 4 * D_MODEL
EPS = 1e-6
IN_WIDTH = 3 * MOBA_WIDTH + MLA_Q_LORA + MLA_KV_LORA + MLA_ROPE + 2 * D_MODEL

kernel_name = "hybrid_moba_mla_gated_block"


def _rms_norm(x, g):
    xf = x.astype(jnp.float32)
    y = xf * lax.rsqrt(jnp.mean(xf * xf, axis=-1, keepdims=True) + EPS)
    return (y * g.astype(jnp.float32)).astype(x.dtype)


def _split_cols(t, widths):
    out, start = [], 0
    for w in widths:
        out.append(t[..., start:start + w])
        start += w
    return out


def _rope(x, positions):
    r = x.shape[-1]
    half = r // 2
    inv_freq = ROPE_THETA ** (-jnp.arange(half, dtype=jnp.float32) * (2.0 / r))
    ang = positions.astype(jnp.float32)[..., None] * inv_freq
    cos = jnp.cos(ang)[:, :, None, :]
    sin = jnp.sin(ang)[:, :, None, :]
    xf = x.astype(jnp.float32)
    x1, x2 = xf[..., :half], xf[..., half:]
    return jnp.concatenate([x1 * cos - x2 * sin, x2 * cos + x1 * sin], axis=-1).astype(x.dtype)


def _partial_rope(x, positions):
    return jnp.concatenate([_rope(x[..., :MOBA_ROT_DIM], positions), x[..., MOBA_ROT_DIM:]], axis=-1)


def _moba_attention(q, k, v):
    B, S, H, Dh = q.shape
    nb = -(-S // MOBA_BLOCK)
    pad = nb * MOBA_BLOCK - S
    topk = min(MOBA_TOPK, nb)
    n_chunks = S // MOBA_Q_CHUNK
    scale = Dh ** -0.5
    qt = jnp.transpose(q, (0, 2, 1, 3))
    kp = jnp.pad(jnp.transpose(k, (0, 2, 1, 3)), ((0, 0), (0, 0), (0, pad), (0, 0)))
    vp = jnp.pad(jnp.transpose(v, (0, 2, 1, 3)), ((0, 0), (0, 0), (0, pad), (0, 0)))
    kb = kp.reshape(B, H, nb, MOBA_BLOCK, Dh)
    vb = vp.reshape(B, H, nb, MOBA_BLOCK, Dh)
    kmean = jnp.mean(kb.astype(jnp.float32), axis=3)
    head_idx = jnp.arange(H)[:, None, None]

    def per_batch(args):
        qb, kbb, vbb, kmb = args

        def per_chunk(c):
            q0 = c * MOBA_Q_CHUNK
            qc = lax.dynamic_slice_in_dim(qb, q0, MOBA_Q_CHUNK, axis=1)
            own = q0 // MOBA_BLOCK
            gate = jnp.einsum('hqd,hnd->hqn', qc.astype(jnp.float32), kmb)
            gate = jnp.where(jnp.arange(nb)[None, None, :] < own, gate, -jnp.inf)
            _, idx = lax.top_k(gate, topk)
            valid = jnp.arange(topk) < own
            kg = kbb[head_idx, idx]
            vg = vbb[head_idx, idx]
            s_past = jnp.einsum('hqd,hqnkd->hqnk', qc, kg).astype(jnp.float32) * scale
            s_past = jnp.where(valid[None, None, :, None], s_past, -jnp.inf)
            s_past = s_past.reshape(H, MOBA_Q_CHUNK, topk * MOBA_BLOCK)
            k_own = lax.dynamic_index_in_dim(kbb, own, axis=1, keepdims=False)
            v_own = lax.dynamic_index_in_dim(vbb, own, axis=1, keepdims=False)
            s_own = jnp.einsum('hqd,hkd->hqk', qc, k_own).astype(jnp.float32) * scale
            qpos = q0 + jnp.arange(MOBA_Q_CHUNK)
            kpos = own * MOBA_BLOCK + jnp.arange(MOBA_BLOCK)
            s_own = jnp.where(kpos[None, None, :] <= qpos[None, :, None], s_own, -jnp.inf)
            p = jax.nn.softmax(jnp.concatenate([s_past, s_own], axis=-1), axis=-1).astype(v.dtype)
            p_past = p[..., :topk * MOBA_BLOCK].reshape(H, MOBA_Q_CHUNK, topk, MOBA_BLOCK)
            p_own = p[..., topk * MOBA_BLOCK:]
            return (jnp.einsum('hqnk,hqnkd->hqd', p_past, vg)
                    + jnp.einsum('hqk,hkd->hqd', p_own, v_own))

        outs = lax.map(per_chunk, jnp.arange(n_chunks))
        return jnp.transpose(outs, (1, 0, 2, 3)).reshape(H, S, Dh)

    o = lax.map(per_batch, (qt, kb, vb, kmean))
    return jnp.transpose(o, (0, 2, 1, 3))


def _mla_attention(q_nope, q_rope, k_nope, k_rope, v):
    S = q_nope.shape[1]
    scale = (MLA_NOPE + MLA_ROPE) ** -0.5
    outs = []
    for c in range(S // MLA_Q_BLOCK):
        q0, q1 = c * MLA_Q_BLOCK, (c + 1) * MLA_Q_BLOCK
        s = (jnp.einsum('bqhd,bkhd->bhqk', q_nope[:, q0:q1], k_nope[:, :q1])
             + jnp.einsum('bqhr,bkr->bhqk', q_rope[:, q0:q1], k_rope[:, :q1]))
        s = s.astype(jnp.float32) * scale
        mask = jnp.arange(q1)[None, :] <= (q0 + jnp.arange(MLA_Q_BLOCK))[:, None]
        s = jnp.where(mask[None, None], s, -jnp.inf)
        p = jax.nn.softmax(s, axis=-1).astype(v.dtype)
        outs.append(jnp.einsum('bhqk,bkhd->bqhd', p, v[:, :q1]))
    return jnp.concatenate(outs, axis=1)


def setup_inputs(seed: int = 0) -> dict:
    key = jax.random.key(seed)
    ks = jax.random.split(key, 20)
    f32 = jnp.float32

    def w(k, shape, fan_in):
        return jax.random.normal(k, shape, f32) * (fan_in ** -0.5)

    def gain(k, n):
        return 1.0 + 0.1 * jax.random.normal(k, (DEPTH, n), f32)

    x = jax.random.normal(ks[0], (BATCH, SEQ, D_MODEL), f32)
    offsets = jax.random.randint(ks[1], (BATCH, 1), 0, 4096)
    positions = (offsets + jnp.arange(SEQ)[None, :]).astype(jnp.int32)
    return {
        "x": x,
        "positions": positions,
        "g_pre_mix": gain(ks[2], D_MODEL),
        "w_in": w(ks[3], (DEPTH, D_MODEL, IN_WIDTH), D_MODEL),
        "b_gate": 0.01 * jax.random.normal(ks[4], (DEPTH, 2 * D_MODEL), f32),
        "g_q_norm": gain(ks[5], MLA_Q_LORA),
        "w_uq": w(ks[6], (DEPTH, MLA_Q_LORA, MLA_HEADS * (MLA_NOPE + MLA_ROPE)), MLA_Q_LORA),
        "g_kv_norm": gain(ks[7], MLA_KV_LORA),
        "w_ukv": w(ks[8], (DEPTH, MLA_KV_LORA, MLA_HEADS * (MLA_NOPE + MLA_V)), MLA_KV_LORA),
        "w_branch_a": w(ks[9], (DEPTH, MOBA_WIDTH, D_MODEL), MOBA_WIDTH),
        "w_branch_b": w(ks[10], (DEPTH, MLA_WIDTH, D_MODEL), MLA_WIDTH),
        "w_out": w(ks[11], (DEPTH, D_MODEL, D_MODEL), D_MODEL),
        "g_post_mix": gain(ks[12], D_MODEL),
        "g_pre_mlp": gain(ks[13], D_MODEL),
        "w_up": w(ks[14], (DEPTH, D_MODEL, D_FF), D_MODEL),
        "w_down": w(ks[15], (DEPTH, D_FF, D_MODEL), D_FF),
        "g_post_mlp": gain(ks[16], D_MODEL),
    }


def reference(x, positions, g_pre_mix, w_in, b_gate, g_q_norm, w_uq, g_kv_norm, w_ukv,
              w_branch_a, w_branch_b, w_out, g_post_mix, g_pre_mlp, w_up, w_down, g_post_mlp):
    B, S, _ = x.shape
    for l in range(DEPTH):
        h = _rms_norm(x, g_pre_mix[l])
        proj = h @ w_in[l]
        qa, ka, va, c_q, c_kv, k_r, gate_logits = _split_cols(
            proj, [MOBA_WIDTH, MOBA_WIDTH, MOBA_WIDTH, MLA_Q_LORA, MLA_KV_LORA, MLA_ROPE, 2 * D_MODEL])
        qa = _partial_rope(qa.reshape(B, S, MOBA_HEADS, MOBA_HEAD_DIM), positions)
        ka = _partial_rope(ka.reshape(B, S, MOBA_HEADS, MOBA_HEAD_DIM), positions)
        va = va.reshape(B, S, MOBA_HEADS, MOBA_HEAD_DIM)
        o_a = _moba_attention(qa, ka, va).reshape(B, S, MOBA_WIDTH) @ w_branch_a[l]
        q = (_rms_norm(c_q, g_q_norm[l]) @ w_uq[l]).reshape(B, S, MLA_HEADS, MLA_NOPE + MLA_ROPE)
        q_nope = q[..., :MLA_NOPE]
        q_rope = _rope(q[..., MLA_NOPE:], positions)
        kv = (_rms_norm(c_kv, g_kv_norm[l]) @ w_ukv[l]).reshape(B, S, MLA_HEADS, MLA_NOPE + MLA_V)
        k_nope, v_b = kv[..., :MLA_NOPE], kv[..., MLA_NOPE:]
        k_rope = _rope(k_r[:, :, None, :], positions)[:, :, 0, :]
        o_b = _mla_attention(q_nope, q_rope, k_nope, k_rope, v_b).reshape(B, S, MLA_WIDTH) @ w_branch_b[l]
        gates = jax.nn.sigmoid((gate_logits + b_gate[l]).astype(jnp.float32)).astype(x.dtype)
        merged = gates[..., :D_MODEL] * o_a + gates[..., D_MODEL:] * o_b
        x = x + _rms_norm(merged @ w_out[l], g_post_mix[l])
        h2 = _rms_norm(x, g_pre_mlp[l])
        m = jnp.square(jax.nn.relu(h2 @ w_up[l])) @ w_down[l]
        x = x + _rms_norm(m, g_post_mlp[l])
    return x
```

```python
import functools

import jax
import jax.numpy as jnp
from jax import lax
from jax.experimental import pallas as pl
from jax.experimental.pallas import tpu as pltpu

F32 = jnp.float32
BF16 = jnp.bfloat16

D_MODEL = 1024
HEADS = 8
HEAD_DIM = 64
MOBA_WIDTH = HEADS * HEAD_DIM
MOBA_ROT_DIM = HEAD_DIM // 4
MOBA_TOPK = 3
MLA_ROPE = 32
MLA_Q_LORA = 384
MLA_KV_LORA = 256
ROPE_THETA = 500000.0
D_FF = 4 * D_MODEL
EPS = 1e-6

LANES = 128
KV_TILE = 256
NEG = -1e30
MLA_ROPE_LANE = HEAD_DIM

TM_PROJ = 512
TM_MIX = 512
TM_MLP = 512
VMEM_LIMIT = 56 * 1024 * 1024

_NT = (((1,), (1,)), ((), ()))


def _rms(x, g):
    y = x * lax.rsqrt(jnp.mean(x * x, axis=-1, keepdims=True) + EPS)
    return y * g


def _const_spec(shape):
    nd = len(shape)
    return pl.BlockSpec(shape, lambda *_: (0,) * nd, pipeline_mode=pl.Buffered(1))


def _rope_group(x, cos, sin_lo, sin_hi, shift):
    return (x * cos
            + pltpu.roll(x, LANES - shift, 1) * sin_lo
            + pltpu.roll(x, shift, 1) * sin_hi)


def _proj_kernel(pos_ref, x_ref, g1_ref, w1_ref, gq_ref, wuq_ref, gkv_ref, wukv_ref,
                 invfa_ref, invfb_ref, sgn_ref,
                 qa_ref, ka_ref, va_ref, qm_ref, km_ref, vb_ref):
    x = x_ref[...]
    h = _rms(x, g1_ref[...]).astype(BF16)
    p = jnp.dot(h, w1_ref[...], preferred_element_type=F32)

    pos = pos_ref[0].astype(F32)
    ang_a = invfa_ref[...] * pos
    ang_b = invfb_ref[...] * pos
    cos_a = jnp.cos(ang_a).T
    sin_a = jnp.sin(ang_a).T
    cos_b = jnp.cos(ang_b).T
    sin_b = jnp.sin(ang_b).T
    sgn = sgn_ref[...]
    sin_a_lo = sin_a * sgn[0:1, :]
    sin_a_hi = sin_a * sgn[1:2, :]
    sin_b_lo = sin_b * sgn[2:3, :]
    sin_b_hi = sin_b * sgn[3:4, :]

    half_a = MOBA_ROT_DIM // 2
    half_b = MLA_ROPE // 2
    sa = HEAD_DIM ** -0.5
    sb = (HEAD_DIM + MLA_ROPE) ** -0.5
    o_k, o_v, o_cq, o_ckv, o_kr = MOBA_WIDTH, 2 * MOBA_WIDTH, 3 * MOBA_WIDTH, \
        3 * MOBA_WIDTH + MLA_Q_LORA, 3 * MOBA_WIDTH + MLA_Q_LORA + MLA_KV_LORA

    for g in range(MOBA_WIDTH // LANES):
        sl = slice(g * LANES, (g + 1) * LANES)
        xq = p[:, g * LANES:(g + 1) * LANES]
        xk = p[:, o_k + g * LANES:o_k + (g + 1) * LANES]
        qa_ref[:, sl] = (_rope_group(xq, cos_a, sin_a_lo, sin_a_hi, half_a) * sa).astype(BF16)
        ka_ref[:, sl] = _rope_group(xk, cos_a, sin_a_lo, sin_a_hi, half_a).astype(BF16)
    va_ref[...] = p[:, o_v:o_cq].astype(BF16)

    cq = _rms(p[:, o_cq:o_ckv], gq_ref[...]).astype(BF16)
    qm = jnp.dot(cq, wuq_ref[...], preferred_element_type=F32)
    ckv = _rms(p[:, o_ckv:o_kr], gkv_ref[...]).astype(BF16)
    kv = jnp.dot(ckv, wukv_ref[...], preferred_element_type=F32)
    kr = _rope_group(p[:, o_kr:o_kr + LANES], cos_b, sin_b_lo, sin_b_hi, half_b)
    for g in range(HEADS):
        sl = slice(g * LANES, (g + 1) * LANES)
        qg = _rope_group(qm[:, g * LANES:(g + 1) * LANES], cos_b, sin_b_lo, sin_b_hi, half_b)
        qm_ref[:, sl] = (qg * sb).astype(BF16)
        km_ref[:, sl] = (kv[:, g * LANES:(g + 1) * LANES] + kr).astype(BF16)
    vb_ref[...] = kv[:, HEADS * LANES:].astype(BF16)


def _proj_call(pos3, x2, g1, w1, gq, wuq, gkv, wukv, invfa, invfb, sgn):
    n = x2.shape[0]
    tm = TM_PROJ
    row = lambda w: pl.BlockSpec((tm, w), lambda i: (i, 0))
    out_w = (MOBA_WIDTH, MOBA_WIDTH, MOBA_WIDTH, HEADS * LANES, HEADS * LANES, MOBA_WIDTH)
    return pl.pallas_call(
        _proj_kernel,
        grid=(n // tm,),
        in_specs=[pl.BlockSpec((1, 1, tm), lambda i: (i, 0, 0)), row(D_MODEL),
                  _const_spec(g1.shape), _const_spec(w1.shape), _const_spec(gq.shape),
                  _const_spec(wuq.shape), _const_spec(gkv.shape), _const_spec(wukv.shape),
                  _const_spec(invfa.shape), _const_spec(invfb.shape), _const_spec(sgn.shape)],
        out_specs=[row(w) for w in out_w],
        out_shape=[jax.ShapeDtypeStruct((n, w), BF16) for w in out_w],
        compiler_params=pltpu.CompilerParams(
            dimension_semantics=("arbitrary",), vmem_limit_bytes=VMEM_LIMIT),
        name="proj",
    )(pos3, x2, g1, w1, gq, wuq, gkv, wukv, invfa, invfb, sgn)


def _attn_kernel(q_ref, k_ref, v_ref, o_ref, kh_ref, vt_ref, sc_ref, acc_ref, *, moba):
    seq = q_ref.shape[1]
    n_tiles = seq // KV_TILE
    t = KV_TILE

    if moba:
        lane = lax.broadcasted_iota(jnp.int32, (seq, LANES), 1)
        kk = k_ref[0]
        zero = jnp.zeros_like(kk)
        kh_ref[0] = jnp.where(lane < HEAD_DIM, kk, zero)
        kh_ref[1] = jnp.where(lane >= HEAD_DIM, kk, zero)
    else:
        kh_ref[0] = k_ref[0, :, 0:LANES]
        kh_ref[1] = k_ref[0, :, LANES:2 * LANES]
    for n in range(n_tiles):
        vt_ref[n] = v_ref[0, n * t:(n + 1) * t, :].astype(F32).T.astype(BF16)

    row8 = lax.broadcasted_iota(jnp.int32, (8, t), 0)
    krow = lax.broadcasted_iota(jnp.int32, (t, t), 0)
    qcol = lax.broadcasted_iota(jnp.int32, (t, t), 1)
    causal = krow <= qcol

    def q_tile(j, _):
        q0 = pl.multiple_of(j * t, t)
        if moba:
            qq = q_ref[0, pl.ds(q0, t), :]
            qh = (qq, qq)
        else:
            qh = (q_ref[0, pl.ds(q0, t), 0:LANES], q_ref[0, pl.ds(q0, t), LANES:2 * LANES])

        def scores(hh, n):
            k0 = pl.multiple_of(n * t, t)
            return lax.dot_general(kh_ref[hh, pl.ds(k0, t), :], qh[hh], _NT,
                                   preferred_element_type=F32)

        def p1(n, carry):
            out = []
            for hh in range(2):
                gate, cmax = carry[hh]
                s = scores(hh, n)
                sc_ref[hh, n] = s
                here = row8 == n
                cmax = jnp.where(here, jnp.max(s, axis=0, keepdims=True), cmax)
                if moba:
                    gate = jnp.where(here, jnp.sum(s, axis=0, keepdims=True), gate)
                out.append((gate, cmax))
            return tuple(out)

        init = tuple((jnp.zeros((8, t), F32), jnp.full((8, t), NEG, F32)) for _ in range(2))
        stats = lax.fori_loop(0, j, p1, init)

        cvec, lsum = [], []
        for hh in range(2):
            gate, cmax = stats[hh]
            if moba:
                rank = jnp.zeros((8, t), jnp.int32)
                for r in range(1, 8):
                    g_r = pltpu.roll(gate, r, 0)
                    m_idx = jnp.where(row8 >= r, row8 - r, row8 - r + 8)
                    beats = (g_r > gate) | ((g_r == gate) & (m_idx < row8))
                    rank = rank + jnp.where((m_idx < j) & beats, 1, 0)
                sel = (row8 < j) & (rank < MOBA_TOPK)
                bias = jnp.where(sel, 0.0, NEG)
            else:
                bias = jnp.where(row8 < j, 0.0, NEG)
            s_d = jnp.where(causal, scores(hh, j), NEG)
            sc_ref[hh, n_tiles] = s_d
            m = jnp.maximum(jnp.max(cmax + bias, axis=0, keepdims=True),
                            jnp.max(s_d, axis=0, keepdims=True))
            cvec.append(bias - m)
            p_d = jnp.exp(s_d - m)
            lsum.append(jnp.sum(p_d, axis=0, keepdims=True))
            acc_ref[hh] = jnp.dot(vt_ref[j, hh * HEAD_DIM:(hh + 1) * HEAD_DIM, :],
                                  p_d.astype(BF16), preferred_element_type=F32)

        def p2(n, ls):
            out = []
            for hh in range(2):
                c_n = jnp.sum(jnp.where(row8 == n, cvec[hh], 0.0), axis=0, keepdims=True)
                p = jnp.exp(sc_ref[hh, n] + c_n)
                out.append(ls[hh] + jnp.sum(p, axis=0, keepdims=True))
                acc_ref[hh] += jnp.dot(vt_ref[n, hh * HEAD_DIM:(hh + 1) * HEAD_DIM, :],
                                       p.astype(BF16), preferred_element_type=F32)
            return tuple(out)

        lsum = lax.fori_loop(0, j, p2, tuple(lsum))
        o_t = jnp.concatenate([acc_ref[hh] / lsum[hh] for hh in range(2)], axis=0)
        o_ref[0, pl.ds(q0, t), :] = o_t.T.astype(o_ref.dtype)
        return 0

    lax.fori_loop(0, n_tiles, q_tile, 0)


def _attn_call(q, k, v, *, moba):
    b, s, _ = q.shape
    qk_w = LANES if moba else 2 * LANES
    n_tiles = s // KV_TILE
    spec = lambda w: pl.BlockSpec((1, s, w), lambda i, h: (i, 0, h))
    return pl.pallas_call(
        functools.partial(_attn_kernel, moba=moba),
        grid=(b, HEADS // 2),
        in_specs=[spec(qk_w), spec(qk_w), spec(LANES)],
        out_specs=spec(LANES),
        out_shape=jax.ShapeDtypeStruct((b, s, HEADS * HEAD_DIM), BF16),
        scratch_shapes=[pltpu.VMEM((2, s, LANES), BF16),
                        pltpu.VMEM((n_tiles, LANES, KV_TILE), BF16),
                        pltpu.VMEM((2, n_tiles + 1, KV_TILE, KV_TILE), F32),
                        pltpu.VMEM((2, HEAD_DIM, KV_TILE), F32)],
        compiler_params=pltpu.CompilerParams(
            dimension_semantics=("arbitrary", "arbitrary"), vmem_limit_bytes=VMEM_LIMIT),
        name="moba_attn" if moba else "mla_attn",
    )(q, k, v)


def _mix_kernel(x_ref, oa_ref, ob_ref, g1_ref, wg_ref, bg_ref, wa_ref, wb_ref, wo_ref, g2_ref,
                out_ref):
    x = x_ref[...]
    h = _rms(x, g1_ref[...]).astype(BF16)
    logits = jnp.dot(h, wg_ref[...], preferred_element_type=F32) + bg_ref[...]
    gates = jax.nn.sigmoid(logits)
    a = jnp.dot(oa_ref[...], wa_ref[...], preferred_element_type=F32)
    b = jnp.dot(ob_ref[...], wb_ref[...], preferred_element_type=F32)
    merged = gates[:, :D_MODEL] * a + gates[:, D_MODEL:] * b
    y = jnp.dot(merged.astype(BF16), wo_ref[...], preferred_element_type=F32)
    out_ref[...] = x + _rms(y, g2_ref[...])


def _mix_call(x2, oa, ob, g1, wg, bg, wa, wb, wo, g2):
    n = x2.shape[0]
    tm = TM_MIX
    row = lambda w: pl.BlockSpec((tm, w), lambda i: (i, 0))
    consts = (g1, wg, bg, wa, wb, wo, g2)
    return pl.pallas_call(
        _mix_kernel,
        grid=(n // tm,),
        in_specs=[row(D_MODEL), row(MOBA_WIDTH), row(MOBA_WIDTH)]
        + [_const_spec(c.shape) for c in consts],
        out_specs=row(D_MODEL),
        out_shape=jax.ShapeDtypeStruct((n, D_MODEL), F32),
        compiler_params=pltpu.CompilerParams(
            dimension_semantics=("arbitrary",), vmem_limit_bytes=VMEM_LIMIT),
        name="mix",
    )(x2, oa, ob, *consts)


def _mlp_kernel(x_ref, g3_ref, wu_ref, wd_ref, g4_ref, out_ref):
    x = x_ref[...]
    h = _rms(x, g3_ref[...]).astype(BF16)
    u = jnp.dot(h, wu_ref[...], preferred_element_type=F32)
    u = jnp.square(jnp.maximum(u, 0.0)).astype(BF16)
    m = jnp.dot(u, wd_ref[...], preferred_element_type=F32)
    out_ref[...] = x + _rms(m, g4_ref[...])


def _mlp_call(x2, g3, wu, wd, g4):
    n = x2.shape[0]
    tm = TM_MLP
    row = pl.BlockSpec((tm, D_MODEL), lambda i: (i, 0))
    consts = (g3, wu, wd, g4)
    return pl.pallas_call(
        _mlp_kernel,
        grid=(n // tm,),
        in_specs=[row] + [_const_spec(c.shape) for c in consts],
        out_specs=row,
        out_shape=jax.ShapeDtypeStruct((n, D_MODEL), F32),
        compiler_params=pltpu.CompilerParams(
            dimension_semantics=("arbitrary",), vmem_limit_bytes=VMEM_LIMIT),
        name="mlp",
    )(x2, *consts)


def _rope_constants():
    lane = jnp.arange(LANES)
    half_a = MOBA_ROT_DIM // 2
    half_b = MLA_ROPE // 2
    inv_a = ROPE_THETA ** (-jnp.arange(half_a, dtype=F32) * (2.0 / MOBA_ROT_DIM))
    inv_b = ROPE_THETA ** (-jnp.arange(half_b, dtype=F32) * (2.0 / MLA_ROPE))
    d = lane % HEAD_DIM
    invf_a = jnp.where(d < MOBA_ROT_DIM, inv_a[d % half_a], 0.0)
    e = lane - MLA_ROPE_LANE
    in_b = (e >= 0) & (e < MLA_ROPE)
    invf_b = jnp.where(in_b, inv_b[e % half_b], 0.0)
    sgn = jnp.stack([
        jnp.where(d < half_a, -1.0, 0.0),
        jnp.where((d >= half_a) & (d < MOBA_ROT_DIM), 1.0, 0.0),
        jnp.where((e >= 0) & (e < half_b), -1.0, 0.0),
        jnp.where((e >= half_b) & (e < MLA_ROPE), 1.0, 0.0),
    ]).astype(F32)
    return invf_a.astype(F32)[:, None], invf_b.astype(F32)[:, None], sgn


def _layer_weights(w_in, w_uq, w_ukv):
    o_kr = 3 * MOBA_WIDTH + MLA_Q_LORA + MLA_KV_LORA
    o_gate = o_kr + MLA_ROPE
    w_kr = jnp.pad(w_in[:, o_kr:o_gate],
                   ((0, 0), (MLA_ROPE_LANE, LANES - MLA_ROPE_LANE - MLA_ROPE)))
    w1 = jnp.concatenate([w_in[:, :o_kr], w_kr], axis=1).astype(BF16)
    wg = w_in[:, o_gate:].astype(BF16)
    pad_head = lambda w: jnp.pad(w, ((0, 0), (0, 0), (0, LANES - w.shape[-1]))).reshape(
        w.shape[0], HEADS * LANES)
    wuq = pad_head(w_uq.reshape(MLA_Q_LORA, HEADS, HEAD_DIM + MLA_ROPE)).astype(BF16)
    ukv = w_ukv.reshape(MLA_KV_LORA, HEADS, 2 * HEAD_DIM)
    wukv = jnp.concatenate(
        [pad_head(ukv[..., :HEAD_DIM]), ukv[..., HEAD_DIM:].reshape(MLA_KV_LORA, HEADS * HEAD_DIM)],
        axis=1).astype(BF16)
    return w1, wg, wuq, wukv


def kernel(x, positions, g_pre_mix, w_in, b_gate, g_q_norm, w_uq, g_kv_norm, w_ukv,
           w_branch_a, w_branch_b, w_out, g_post_mix, g_pre_mlp, w_up, w_down, g_post_mlp):
    b, s, d = x.shape
    n = b * s
    assert d == D_MODEL and s % KV_TILE == 0 and n % TM_PROJ == 0
    invfa, invfb, sgn = _rope_constants()
    pos3 = positions.reshape(n // TM_PROJ, 1, TM_PROJ)
    x2 = x.reshape(n, d)
    row = lambda v: v[None, :]
    for l in range(w_in.shape[0]):
        w1, wg, wuq, wukv = _layer_weights(w_in[l], w_uq[l], w_ukv[l])
        qa, ka, va, qm, km, vb = _proj_call(
            pos3, x2, row(g_pre_mix[l]), w1, row(g_q_norm[l]), wuq, row(g_kv_norm[l]), wukv,
            invfa, invfb, sgn)
        o_a = _attn_call(qa.reshape(b, s, -1), ka.reshape(b, s, -1), va.reshape(b, s, -1),
                         moba=True)
        o_b = _attn_call(qm.reshape(b, s, -1), km.reshape(b, s, -1), vb.reshape(b, s, -1),
                         moba=False)
        x2 = _mix_call(x2, o_a.reshape(n, -1), o_b.reshape(n, -1), row(g_pre_mix[l]), wg,
                       row(b_gate[l]), w_branch_a[l].astype(BF16), w_branch_b[l].astype(BF16),
                       w_out[l].astype(BF16), row(g_post_mix[l]))
        x2 = _mlp_call(x2, row(g_pre_mlp[l]), w_up[l].astype(BF16), w_down[l].astype(BF16),
                       row(g_post_mlp[l]))
    return x2.reshape(b, s, d)
```

```python
import functools

import jax
import jax.numpy as jnp
from jax import lax
from jax.experimental import pallas as pl
from jax.experimental.pallas import tpu as pltpu

F32 = jnp.float32
BF16 = jnp.bfloat16

D_MODEL = 1024
HEADS = 8
HEAD_DIM = 64
MOBA_WIDTH = HEADS * HEAD_DIM
MOBA_ROT_DIM = HEAD_DIM // 4
MOBA_TOPK = 3
MLA_ROPE = 32
MLA_Q_LORA = 384
MLA_KV_LORA = 256
ROPE_THETA = 500000.0
D_FF = 4 * D_MODEL
EPS = 1e-6

LANES = 128
KV_TILE = 256
NEG = -1e30
LOG2E = 1.4426950408889634
MLA_ROPE_LANE = HEAD_DIM

TM_PROJ = 512
TM_MIX = 512
TM_MLP = 512
VMEM_LIMIT = 56 * 1024 * 1024

_NT = (((1,), (1,)), ((), ()))


def _rms(x, g):
    y = x * lax.rsqrt(jnp.mean(x * x, axis=-1, keepdims=True) + EPS)
    return y * g


def _const_spec(shape):
    nd = len(shape)
    return pl.BlockSpec(shape, lambda *_: (0,) * nd, pipeline_mode=pl.Buffered(1))


def _rope_group(x, cos, sin_lo, sin_hi, shift):
    return (x * cos
            + pltpu.roll(x, LANES - shift, 1) * sin_lo
            + pltpu.roll(x, shift, 1) * sin_hi)


def _proj_kernel(pos_ref, x_ref, g1_ref, w1_ref, gq_ref, wuq_ref, gkv_ref, wukv_ref,
                 invfa_ref, invfb_ref, sgn_ref,
                 qa_ref, ka_ref, va_ref, qm_ref, km_ref, vb_ref):
    x = x_ref[...]
    h = _rms(x, g1_ref[...]).astype(BF16)
    p = jnp.dot(h, w1_ref[...], preferred_element_type=F32)

    pos = pos_ref[0].astype(F32)
    ang_a = invfa_ref[...] * pos
    ang_b = invfb_ref[...] * pos
    cos_a = jnp.cos(ang_a).T
    sin_a = jnp.sin(ang_a).T
    cos_b = jnp.cos(ang_b).T
    sin_b = jnp.sin(ang_b).T
    sgn = sgn_ref[...]
    sin_a_lo = sin_a * sgn[0:1, :]
    sin_a_hi = sin_a * sgn[1:2, :]
    sin_b_lo = sin_b * sgn[2:3, :]
    sin_b_hi = sin_b * sgn[3:4, :]

    half_a = MOBA_ROT_DIM // 2
    half_b = MLA_ROPE // 2
    sa = HEAD_DIM ** -0.5 * LOG2E
    sb = (HEAD_DIM + MLA_ROPE) ** -0.5 * LOG2E
    o_k, o_v, o_cq, o_ckv, o_kr = MOBA_WIDTH, 2 * MOBA_WIDTH, 3 * MOBA_WIDTH, \
        3 * MOBA_WIDTH + MLA_Q_LORA, 3 * MOBA_WIDTH + MLA_Q_LORA + MLA_KV_LORA

    for g in range(MOBA_WIDTH // LANES):
        sl = slice(g * LANES, (g + 1) * LANES)
        xq = p[:, g * LANES:(g + 1) * LANES]
        xk = p[:, o_k + g * LANES:o_k + (g + 1) * LANES]
        qa_ref[:, sl] = (_rope_group(xq, cos_a, sin_a_lo, sin_a_hi, half_a) * sa).astype(BF16)
        ka_ref[:, sl] = _rope_group(xk, cos_a, sin_a_lo, sin_a_hi, half_a).astype(BF16)
    va_ref[...] = p[:, o_v:o_cq].astype(BF16)

    cq = _rms(p[:, o_cq:o_ckv], gq_ref[...]).astype(BF16)
    qm = jnp.dot(cq, wuq_ref[...], preferred_element_type=F32)
    ckv = _rms(p[:, o_ckv:o_kr], gkv_ref[...]).astype(BF16)
    kv = jnp.dot(ckv, wukv_ref[...], preferred_element_type=F32)
    kr = _rope_group(p[:, o_kr:o_kr + LANES], cos_b, sin_b_lo, sin_b_hi, half_b)
    for g in range(HEADS):
        sl = slice(g * LANES, (g + 1) * LANES)
        qg = _rope_group(qm[:, g * LANES:(g + 1) * LANES], cos_b, sin_b_lo, sin_b_hi, half_b)
        qm_ref[:, sl] = (qg * sb).astype(BF16)
        km_ref[:, sl] = (kv[:, g * LANES:(g + 1) * LANES] + kr).astype(BF16)
    vb_ref[...] = kv[:, HEADS * LANES:].astype(BF16)


def _proj_call(pos3, x2, g1, w1, gq, wuq, gkv, wukv, invfa, invfb, sgn):
    n = x2.shape[0]
    tm = TM_PROJ
    row = lambda w: pl.BlockSpec((tm, w), lambda i: (i, 0))
    out_w = (MOBA_WIDTH, MOBA_WIDTH, MOBA_WIDTH, HEADS * LANES, HEADS * LANES, MOBA_WIDTH)
    return pl.pallas_call(
        _proj_kernel,
        grid=(n // tm,),
        in_specs=[pl.BlockSpec((1, 1, tm), lambda i: (i, 0, 0)), row(D_MODEL),
                  _const_spec(g1.shape), _const_spec(w1.shape), _const_spec(gq.shape),
                  _const_spec(wuq.shape), _const_spec(gkv.shape), _const_spec(wukv.shape),
                  _const_spec(invfa.shape), _const_spec(invfb.shape), _const_spec(sgn.shape)],
        out_specs=[row(w) for w in out_w],
        out_shape=[jax.ShapeDtypeStruct((n, w), BF16) for w in out_w],
        compiler_params=pltpu.CompilerParams(
            dimension_semantics=("arbitrary",), vmem_limit_bytes=VMEM_LIMIT),
        name="proj",
    )(pos3, x2, g1, w1, gq, wuq, gkv, wukv, invfa, invfb, sgn)


GATE_ROWS = 16
ONES_ROWS = 16


def _attn_kernel(q_ref, k_ref, v_ref, o_ref, kx_ref, vt_ref, p_ref, *, moba):
    seq, width = q_ref.shape[1], q_ref.shape[2]
    n_tiles = seq // KV_TILE
    t = KV_TILE
    head_lanes = width // 2
    g0 = GATE_ROWS if moba else 0

    vt_ref[2 * HEAD_DIM:, :] = jnp.ones((ONES_ROWS, seq), BF16)
    for n in range(n_tiles):
        cols = slice(n * t, (n + 1) * t)
        vt_ref[0:2 * HEAD_DIM, cols] = v_ref[0, cols, :].astype(F32).T.astype(BF16)
    if moba:
        row8w = lax.broadcasted_iota(jnp.int32, (8, width), 0)
        kmean = jnp.zeros((8, width), F32)
        for n in range(n_tiles):
            kb = k_ref[0, n * t:(n + 1) * t, :]
            kx_ref[g0 + n * t:g0 + (n + 1) * t, :] = kb
            ksum = jnp.sum(kb.astype(F32), axis=0, keepdims=True)
            kmean = jnp.where(row8w == n, ksum * (1.0 / t), kmean)
        hi = kmean.astype(BF16).astype(F32)
        kx_ref[0:g0, :] = jnp.concatenate([hi, kmean - hi], axis=0).astype(BF16)
        keys = kx_ref
    else:
        keys = k_ref.at[0]

    lane_q = lax.broadcasted_iota(jnp.int32, (t, width), 1)
    row8 = lax.broadcasted_iota(jnp.int32, (8, 2 * t), 0)
    krow = lax.broadcasted_iota(jnp.int32, (t, 2 * t), 0)
    qcol = lax.broadcasted_iota(jnp.int32, (t, 2 * t), 1) & (t - 1)
    causal = krow <= qcol

    def qk(j):
        q = q_ref[0, j * t:(j + 1) * t, :]
        zero = jnp.zeros_like(q)
        q_cat = jnp.concatenate([jnp.where(lane_q < head_lanes, q, zero),
                                 jnp.where(lane_q >= head_lanes, q, zero)], axis=0)
        return lax.dot_general(keys[0:g0 + (j + 1) * t, :], q_cat, _NT,
                               preferred_element_type=F32)

    def softmax_pv(j, s_ext):
        buf = j % 2
        nk = (j + 1) * t
        blocks = [s_ext[g0 + n * t:g0 + (n + 1) * t, :] for n in range(j + 1)]
        cmax = jnp.full((8, 2 * t), NEG, F32)
        for n in range(j):
            cmax = jnp.where(row8 == n, jnp.max(blocks[n], axis=0, keepdims=True), cmax)
        if moba:
            gate = s_ext[0:8, :] + s_ext[8:16, :]
            rank = jnp.zeros((8, 2 * t), jnp.int32)
            for r in range(1, 8):
                g_r = pltpu.roll(gate, r, 0)
                m_idx = jnp.where(row8 >= r, row8 - r, row8 - r + 8)
                beats = (g_r > gate) | ((g_r == gate) & (m_idx < row8))
                rank = rank + jnp.where((m_idx < j) & beats, 1, 0)
            bias = jnp.where((row8 < j) & (rank < MOBA_TOPK), 0.0, NEG)
        else:
            bias = jnp.where(row8 < j, 0.0, NEG)

        s_d = jnp.where(causal, blocks[j], NEG)
        m = jnp.maximum(jnp.max(cmax + bias, axis=0, keepdims=True),
                        jnp.max(s_d, axis=0, keepdims=True))
        cvec = bias - m
        p_ref[buf, j * t:(j + 1) * t, :] = jnp.exp2(s_d - m).astype(BF16)
        for n in range(j):
            p_ref[buf, n * t:(n + 1) * t, :] = jnp.exp2(blocks[n] + cvec[n:n + 1, :]).astype(BF16)
        acc = jnp.dot(vt_ref[:, 0:nk], p_ref[buf, 0:nk, :],
                      preferred_element_type=F32)
        lsum = acc[2 * HEAD_DIM:2 * HEAD_DIM + 1, :]
        o_t = jnp.concatenate([acc[0:HEAD_DIM, 0:t] / lsum[:, 0:t],
                               acc[HEAD_DIM:2 * HEAD_DIM, t:] / lsum[:, t:]], axis=0)
        o_ref[0, j * t:(j + 1) * t, :] = o_t.T.astype(o_ref.dtype)

    s_next = qk(0)
    for j in range(n_tiles):
        s_cur = s_next
        if j + 1 < n_tiles:
            s_next = qk(j + 1)
        softmax_pv(j, s_cur)


def _attn_call(q, k, v, *, moba):
    b, s, _ = q.shape
    qk_w = LANES if moba else 2 * LANES
    spec = lambda w: pl.BlockSpec((1, s, w), lambda i, h: (i, 0, h))
    return pl.pallas_call(
        functools.partial(_attn_kernel, moba=moba),
        grid=(b, HEADS // 2),
        in_specs=[spec(qk_w), spec(qk_w), spec(LANES)],
        out_specs=spec(LANES),
        out_shape=jax.ShapeDtypeStruct((b, s, HEADS * HEAD_DIM), BF16),
        scratch_shapes=[pltpu.VMEM((GATE_ROWS + s, qk_w), BF16),
                        pltpu.VMEM((2 * HEAD_DIM + ONES_ROWS, s), BF16),
                        pltpu.VMEM((2, s, 2 * KV_TILE), BF16)],
        compiler_params=pltpu.CompilerParams(
            dimension_semantics=("arbitrary", "arbitrary"), vmem_limit_bytes=VMEM_LIMIT),
        name="moba_attn" if moba else "mla_attn",
    )(q, k, v)


def _mix_kernel(x_ref, oa_ref, ob_ref, g1_ref, wg_ref, bg_ref, wa_ref, wb_ref, wo_ref, g2_ref,
                out_ref):
    x = x_ref[...]
    h = _rms(x, g1_ref[...]).astype(BF16)
    logits = jnp.dot(h, wg_ref[...], preferred_element_type=F32) + bg_ref[...]
    gates = jax.nn.sigmoid(logits)
    a = jnp.dot(oa_ref[...], wa_ref[...], preferred_element_type=F32)
    b = jnp.dot(ob_ref[...], wb_ref[...], preferred_element_type=F32)
    merged = gates[:, :D_MODEL] * a + gates[:, D_MODEL:] * b
    y = jnp.dot(merged.astype(BF16), wo_ref[...], preferred_element_type=F32)
    out_ref[...] = x + _rms(y, g2_ref[...])


def _mix_call(x2, oa, ob, g1, wg, bg, wa, wb, wo, g2):
    n = x2.shape[0]
    tm = TM_MIX
    row = lambda w: pl.BlockSpec((tm, w), lambda i: (i, 0))
    consts = (g1, wg, bg, wa, wb, wo, g2)
    return pl.pallas_call(
        _mix_kernel,
        grid=(n // tm,),
        in_specs=[row(D_MODEL), row(MOBA_WIDTH), row(MOBA_WIDTH)]
        + [_const_spec(c.shape) for c in consts],
        out_specs=row(D_MODEL),
        out_shape=jax.ShapeDtypeStruct((n, D_MODEL), F32),
        compiler_params=pltpu.CompilerParams(
            dimension_semantics=("arbitrary",), vmem_limit_bytes=VMEM_LIMIT),
        name="mix",
    )(x2, oa, ob, *consts)


def _mlp_kernel(x_ref, g3_ref, wu_ref, wd_ref, g4_ref, out_ref):
    x = x_ref[...]
    h = _rms(x, g3_ref[...]).astype(BF16)
    u = jnp.dot(h, wu_ref[...], preferred_element_type=F32)
    u = jnp.square(jnp.maximum(u, 0.0)).astype(BF16)
    m = jnp.dot(u, wd_ref[...], preferred_element_type=F32)
    out_ref[...] = x + _rms(m, g4_ref[...])


def _mlp_call(x2, g3, wu, wd, g4):
    n = x2.shape[0]
    tm = TM_MLP
    row = pl.BlockSpec((tm, D_MODEL), lambda i: (i, 0))
    consts = (g3, wu, wd, g4)
    return pl.pallas_call(
        _mlp_kernel,
        grid=(n // tm,),
        in_specs=[row] + [_const_spec(c.shape) for c in consts],
        out_specs=row,
        out_shape=jax.ShapeDtypeStruct((n, D_MODEL), F32),
        compiler_params=pltpu.CompilerParams(
            dimension_semantics=("arbitrary",), vmem_limit_bytes=VMEM_LIMIT),
        name="mlp",
    )(x2, *consts)


def _rope_constants():
    lane = jnp.arange(LANES)
    half_a = MOBA_ROT_DIM // 2
    half_b = MLA_ROPE // 2
    inv_a = ROPE_THETA ** (-jnp.arange(half_a, dtype=F32) * (2.0 / MOBA_ROT_DIM))
    inv_b = ROPE_THETA ** (-jnp.arange(half_b, dtype=F32) * (2.0 / MLA_ROPE))
    d = lane % HEAD_DIM
    invf_a = jnp.where(d < MOBA_ROT_DIM, inv_a[d % half_a], 0.0)
    e = lane - MLA_ROPE_LANE
    in_b = (e >= 0) & (e < MLA_ROPE)
    invf_b = jnp.where(in_b, inv_b[e % half_b], 0.0)
    sgn = jnp.stack([
        jnp.where(d < half_a, -1.0, 0.0),
        jnp.where((d >= half_a) & (d < MOBA_ROT_DIM), 1.0, 0.0),
        jnp.where((e >= 0) & (e < half_b), -1.0, 0.0),
        jnp.where((e >= half_b) & (e < MLA_ROPE), 1.0, 0.0),
    ]).astype(F32)
    return invf_a.astype(F32)[:, None], invf_b.astype(F32)[:, None], sgn


def _layer_weights(w_in, w_uq, w_ukv):
    o_kr = 3 * MOBA_WIDTH + MLA_Q_LORA + MLA_KV_LORA
    o_gate = o_kr + MLA_ROPE
    w_kr = jnp.pad(w_in[:, o_kr:o_gate],
                   ((0, 0), (MLA_ROPE_LANE, LANES - MLA_ROPE_LANE - MLA_ROPE)))
    w1 = jnp.concatenate([w_in[:, :o_kr], w_kr], axis=1).astype(BF16)
    wg = w_in[:, o_gate:].astype(BF16)
    pad_head = lambda w: jnp.pad(w, ((0, 0), (0, 0), (0, LANES - w.shape[-1]))).reshape(
        w.shape[0], HEADS * LANES)
    wuq = pad_head(w_uq.reshape(MLA_Q_LORA, HEADS, HEAD_DIM + MLA_ROPE)).astype(BF16)
    ukv = w_ukv.reshape(MLA_KV_LORA, HEADS, 2 * HEAD_DIM)
    wukv = jnp.concatenate(
        [pad_head(ukv[..., :HEAD_DIM]), ukv[..., HEAD_DIM:].reshape(MLA_KV_LORA, HEADS * HEAD_DIM)],
        axis=1).astype(BF16)
    return w1, wg, wuq, wukv


def kernel(x, positions, g_pre_mix, w_in, b_gate, g_q_norm, w_uq, g_kv_norm, w_ukv,
           w_branch_a, w_branch_b, w_out, g_post_mix, g_pre_mlp, w_up, w_down, g_post_mlp):
    b, s, d = x.shape
    n = b * s
    assert d == D_MODEL and s % KV_TILE == 0 and n % TM_PROJ == 0
    invfa, invfb, sgn = _rope_constants()
    pos3 = positions.reshape(n // TM_PROJ, 1, TM_PROJ)
    x2 = x.reshape(n, d)
    row = lambda v: v[None, :]
    for l in range(w_in.shape[0]):
        w1, wg, wuq, wukv = _layer_weights(w_in[l], w_uq[l], w_ukv[l])
        qa, ka, va, qm, km, vb = _proj_call(
            pos3, x2, row(g_pre_mix[l]), w1, row(g_q_norm[l]), wuq, row(g_kv_norm[l]), wukv,
            invfa, invfb, sgn)
        o_a = _attn_call(qa.reshape(b, s, -1), ka.reshape(b, s, -1), va.reshape(b, s, -1),
                         moba=True)
        o_b = _attn_call(qm.reshape(b, s, -1), km.reshape(b, s, -1), vb.reshape(b, s, -1),
                         moba=False)
        x2 = _mix_call(x2, o_a.reshape(n, -1), o_b.reshape(n, -1), row(g_pre_mix[l]), wg,
                       row(b_gate[l]), w_branch_a[l].astype(BF16), w_branch_b[l].astype(BF16),
                       w_out[l].astype(BF16), row(g_post_mix[l]))
        x2 = _mlp_call(x2, row(g_pre_mlp[l]), w_up[l].astype(BF16), w_down[l].astype(BF16),
                       row(g_post_mlp[l]))
    return x2.reshape(b, s, d)
```

```python
import functools

import jax
import jax.numpy as jnp
from jax import lax
from jax.experimental import pallas as pl
from jax.experimental.pallas import tpu as pltpu

F32 = jnp.float32
BF16 = jnp.bfloat16

D_MODEL = 1024
HEADS = 8
HEAD_DIM = 64
MOBA_WIDTH = HEADS * HEAD_DIM
MOBA_ROT_DIM = HEAD_DIM // 4
MOBA_TOPK = 3
MLA_ROPE = 32
MLA_Q_LORA = 384
MLA_KV_LORA = 256
ROPE_THETA = 500000.0
D_FF = 4 * D_MODEL
EPS = 1e-6

LANES = 128
KV_TILE = 256
NEG = -1e30
LOG2E = 1.4426950408889634
MLA_ROPE_LANE = HEAD_DIM

TM_PROJ = 1024
TM_MIX = 512
TM_MLP = 512
ROW_SPLIT = 2
VMEM_LIMIT = 56 * 1024 * 1024

_NT = (((1,), (1,)), ((), ()))


def _rms(x, g):
    y = x * lax.rsqrt(jnp.mean(x * x, axis=-1, keepdims=True) + EPS)
    return y * g


def _const_spec(shape):
    nd = len(shape)
    return pl.BlockSpec(shape, lambda *_: (0,) * nd, pipeline_mode=pl.Buffered(1))


def _rope_group(x, cos, sin_lo, sin_hi, shift):
    return (x * cos
            + pltpu.roll(x, LANES - shift, 1) * sin_lo
            + pltpu.roll(x, shift, 1) * sin_hi)


N_TRIG = 32
N_TABLES = 6
PROJ_SPLIT = 4


def _proj_kernel(pos_ref, x_ref, g1_ref, w1_ref, gq_ref, wuq_ref, gkv_ref, wukv_ref,
                 invf_ref, place_ref,
                 qa_ref, ka_ref, va_ref, qm_ref, km_ref, vb_ref):
    half_a = MOBA_ROT_DIM // 2
    half_b = MLA_ROPE // 2
    o_k, o_v, o_cq, o_ckv, o_kr = MOBA_WIDTH, 2 * MOBA_WIDTH, 3 * MOBA_WIDTH, \
        3 * MOBA_WIDTH + MLA_Q_LORA, 3 * MOBA_WIDTH + MLA_Q_LORA + MLA_KV_LORA
    sub = x_ref.shape[0] // PROJ_SPLIT

    for part in range(PROJ_SPLIT):
        rows = slice(part * sub, (part + 1) * sub)
        x = x_ref[rows, :]
        h = _rms(x, g1_ref[...]).astype(BF16)
        p = jnp.dot(h, w1_ref[...], preferred_element_type=F32)

        ang = invf_ref[...] * pos_ref[0, :, rows].astype(F32)
        trig = jnp.concatenate(
            [jnp.cos(ang), jnp.sin(ang), jnp.zeros((LANES - 2 * N_TRIG, sub), F32)], axis=0).T
        hi = trig.astype(BF16)
        lo = (trig - hi.astype(F32)).astype(BF16)
        tab = jnp.dot(jnp.concatenate([hi, lo], axis=1), place_ref[...],
                      preferred_element_type=F32)
        cos_a, sin_a_lo, sin_a_hi, cos_b, sin_b_lo, sin_b_hi = [
            tab[:, i * LANES:(i + 1) * LANES] for i in range(N_TABLES)]

        for g in range(MOBA_WIDTH // LANES):
            sl = slice(g * LANES, (g + 1) * LANES)
            xq = p[:, g * LANES:(g + 1) * LANES]
            xk = p[:, o_k + g * LANES:o_k + (g + 1) * LANES]
            qa_ref[rows, sl] = _rope_group(xq, cos_a, sin_a_lo, sin_a_hi, half_a).astype(BF16)
            ka_ref[rows, sl] = _rope_group(xk, cos_a, sin_a_lo, sin_a_hi, half_a).astype(BF16)
        va_ref[rows, :] = p[:, o_v:o_cq].astype(BF16)

        cq = _rms(p[:, o_cq:o_ckv], gq_ref[...]).astype(BF16)
        qm = jnp.dot(cq, wuq_ref[...], preferred_element_type=F32)
        ckv = _rms(p[:, o_ckv:o_kr], gkv_ref[...]).astype(BF16)
        kv = jnp.dot(ckv, wukv_ref[...], preferred_element_type=F32)
        kr = _rope_group(p[:, o_kr:o_kr + LANES], cos_b, sin_b_lo, sin_b_hi, half_b)
        for g in range(HEADS):
            sl = slice(g * LANES, (g + 1) * LANES)
            qg = _rope_group(qm[:, g * LANES:(g + 1) * LANES], cos_b, sin_b_lo, sin_b_hi, half_b)
            qm_ref[rows, sl] = qg.astype(BF16)
            km_ref[rows, sl] = (kv[:, g * LANES:(g + 1) * LANES] + kr).astype(BF16)
        vb_ref[rows, :] = kv[:, HEADS * LANES:].astype(BF16)


def _proj_call(pos3, x2, g1, w1, gq, wuq, gkv, wukv, invf, place):
    n = x2.shape[0]
    tm = TM_PROJ
    row = lambda w: pl.BlockSpec((tm, w), lambda i: (i, 0))
    out_w = (MOBA_WIDTH, MOBA_WIDTH, MOBA_WIDTH, HEADS * LANES, HEADS * LANES, MOBA_WIDTH)
    consts = (g1, w1, gq, wuq, gkv, wukv, invf, place)
    return pl.pallas_call(
        _proj_kernel,
        grid=(n // tm,),
        in_specs=[pl.BlockSpec((1, 1, tm), lambda i: (i, 0, 0)), row(D_MODEL)]
        + [_const_spec(c.shape) for c in consts],
        out_specs=[row(w) for w in out_w],
        out_shape=[jax.ShapeDtypeStruct((n, w), BF16) for w in out_w],
        compiler_params=pltpu.CompilerParams(
            dimension_semantics=("arbitrary",), vmem_limit_bytes=VMEM_LIMIT),
        name="proj",
    )(pos3, x2, *consts)


GATE_ROWS = 16
ONES_ROWS = 16

def _attn_kernel(q_ref, k_ref, v_ref, o_ref, kx_ref, vt_ref, p_ref, *, moba):
    seq, width = q_ref.shape[1], q_ref.shape[2]
    n_tiles = seq // KV_TILE
    t = KV_TILE
    head_lanes = width // 2
    g0 = GATE_ROWS if moba else 0

    for hh in range(2):
        vt_ref[hh, HEAD_DIM:, :] = jnp.ones((ONES_ROWS, seq), BF16)
    for n in range(n_tiles):
        cols = slice(n * t, (n + 1) * t)
        v_t = v_ref[0, cols, :].astype(F32).T.astype(BF16)
        for hh in range(2):
            vt_ref[hh, 0:HEAD_DIM, cols] = v_t[hh * HEAD_DIM:(hh + 1) * HEAD_DIM, :]
    if moba:
        row8w = lax.broadcasted_iota(jnp.int32, (8, width), 0)
        kmean = jnp.zeros((8, width), F32)
        for n in range(n_tiles):
            kb = k_ref[0, n * t:(n + 1) * t, :]
            kx_ref[g0 + n * t:g0 + (n + 1) * t, :] = kb
            ksum = jnp.sum(kb.astype(F32), axis=0, keepdims=True)
            kmean = jnp.where(row8w == n, ksum * (1.0 / t), kmean)
        hi = kmean.astype(BF16).astype(F32)
        kx_ref[0:g0, :] = jnp.concatenate([hi, kmean - hi], axis=0).astype(BF16)
        keys = kx_ref
    else:
        keys = k_ref.at[0]

    lane_q = lax.broadcasted_iota(jnp.int32, (t, width), 1)
    row8 = lax.broadcasted_iota(jnp.int32, (8, 2 * t), 0)
    krow = lax.broadcasted_iota(jnp.int32, (t, 2 * t), 0)
    qcol = lax.broadcasted_iota(jnp.int32, (t, 2 * t), 1) & (t - 1)
    causal = krow <= qcol

    def qk(j):
        q = q_ref[0, j * t:(j + 1) * t, :]
        zero = jnp.zeros_like(q)
        q_cat = jnp.concatenate([jnp.where(lane_q < head_lanes, q, zero),
                                 jnp.where(lane_q >= head_lanes, q, zero)], axis=0)
        return lax.dot_general(keys[0:g0 + (j + 1) * t, :], q_cat, _NT,
                               preferred_element_type=F32)

    def softmax_pv(j, s_ext):
        buf = j % 2
        nk = (j + 1) * t
        blocks = [s_ext[g0 + n * t:g0 + (n + 1) * t, :] for n in range(j + 1)]
        cmax = jnp.full((8, 2 * t), NEG, F32)
        for n in range(j):
            cmax = jnp.where(row8 == n, jnp.max(blocks[n], axis=0, keepdims=True), cmax)
        if moba:
            gate = s_ext[0:8, :] + s_ext[8:16, :]
            rank = jnp.zeros((8, 2 * t), jnp.int32)
            for r in range(1, 8):
                g_r = pltpu.roll(gate, r, 0)
                m_idx = jnp.where(row8 >= r, row8 - r, row8 - r + 8)
                beats = (g_r > gate) | ((g_r == gate) & (m_idx < row8))
                rank = rank + jnp.where((m_idx < j) & beats, 1, 0)
            bias = jnp.where((row8 < j) & (rank < MOBA_TOPK), 0.0, NEG)
        else:
            bias = jnp.where(row8 < j, 0.0, NEG)

        s_d = jnp.where(causal, blocks[j], NEG)
        m = jnp.maximum(jnp.max(cmax + bias, axis=0, keepdims=True),
                        jnp.max(s_d, axis=0, keepdims=True))
        cvec = bias - m
        p_ref[buf, j * t:(j + 1) * t, :] = jnp.exp2(s_d - m).astype(BF16)
        for n in range(j):
            p_ref[buf, n * t:(n + 1) * t, :] = jnp.exp2(blocks[n] + cvec[n:n + 1, :]).astype(BF16)
        o_heads = []
        for hh in range(2):
            acc = jnp.dot(vt_ref[hh, :, 0:nk], p_ref[buf, 0:nk, hh * t:(hh + 1) * t],
                          preferred_element_type=F32)
            o_heads.append(acc[0:HEAD_DIM, :] / acc[HEAD_DIM:HEAD_DIM + 1, :])
        o_t = jnp.concatenate(o_heads, axis=0)
        o_ref[0, j * t:(j + 1) * t, :] = o_t.T.astype(o_ref.dtype)

    s_next = qk(0)
    for j in range(n_tiles):
        s_cur = s_next
        if j + 1 < n_tiles:
            s_next = qk(j + 1)
        softmax_pv(j, s_cur)


def _attn_call(q, k, v, *, moba):
    b, s, _ = q.shape
    qk_w = LANES if moba else 2 * LANES
    spec = lambda w: pl.BlockSpec((1, s, w), lambda i, h: (i, 0, h))
    return pl.pallas_call(
        functools.partial(_attn_kernel, moba=moba),
        grid=(b, HEADS // 2),
        in_specs=[spec(qk_w), spec(qk_w), spec(LANES)],
        out_specs=spec(LANES),
        out_shape=jax.ShapeDtypeStruct((b, s, HEADS * HEAD_DIM), BF16),
        scratch_shapes=[pltpu.VMEM((GATE_ROWS + s, qk_w), BF16),
                        pltpu.VMEM((2, HEAD_DIM + ONES_ROWS, s), BF16),
                        pltpu.VMEM((2, s, 2 * KV_TILE), BF16)],
        compiler_params=pltpu.CompilerParams(
            dimension_semantics=("arbitrary", "arbitrary"), vmem_limit_bytes=VMEM_LIMIT),
        name="moba_attn" if moba else "mla_attn",
    )(q, k, v)


def _mix_kernel(x_ref, oa_ref, ob_ref, g1_ref, wg_ref, bg_ref, wa_ref, wb_ref, wo_ref, g2_ref,
                out_ref):
    sub = x_ref.shape[0] // ROW_SPLIT
    for part in range(ROW_SPLIT):
        rows = slice(part * sub, (part + 1) * sub)
        x = x_ref[rows, :]
        h = _rms(x, g1_ref[...]).astype(BF16)
        logits = jnp.dot(h, wg_ref[...], preferred_element_type=F32) + bg_ref[...]
        gates = jax.nn.sigmoid(logits)
        a = jnp.dot(oa_ref[rows, :], wa_ref[...], preferred_element_type=F32)
        b = jnp.dot(ob_ref[rows, :], wb_ref[...], preferred_element_type=F32)
        merged = gates[:, :D_MODEL] * a + gates[:, D_MODEL:] * b
        y = jnp.dot(merged.astype(BF16), wo_ref[...], preferred_element_type=F32)
        out_ref[rows, :] = x + _rms(y, g2_ref[...])


def _mix_call(x2, oa, ob, g1, wg, bg, wa, wb, wo, g2):
    n = x2.shape[0]
    tm = TM_MIX
    row = lambda w: pl.BlockSpec((tm, w), lambda i: (i, 0))
    consts = (g1, wg, bg, wa, wb, wo, g2)
    return pl.pallas_call(
        _mix_kernel,
        grid=(n // tm,),
        in_specs=[row(D_MODEL), row(MOBA_WIDTH), row(MOBA_WIDTH)]
        + [_const_spec(c.shape) for c in consts],
        out_specs=row(D_MODEL),
        out_shape=jax.ShapeDtypeStruct((n, D_MODEL), F32),
        compiler_params=pltpu.CompilerParams(
            dimension_semantics=("arbitrary",), vmem_limit_bytes=VMEM_LIMIT),
        name="mix",
    )(x2, oa, ob, *consts)


def _mlp_kernel(x_ref, g3_ref, wu_ref, wd_ref, g4_ref, out_ref):
    sub = x_ref.shape[0] // ROW_SPLIT
    for part in range(ROW_SPLIT):
        rows = slice(part * sub, (part + 1) * sub)
        x = x_ref[rows, :]
        h = _rms(x, g3_ref[...]).astype(BF16)
        u = jnp.dot(h, wu_ref[...], preferred_element_type=F32)
        u = jnp.square(jnp.maximum(u, 0.0)).astype(BF16)
        m = jnp.dot(u, wd_ref[...], preferred_element_type=F32)
        out_ref[rows, :] = x + _rms(m, g4_ref[...])


def _mlp_call(x2, g3, wu, wd, g4):
    n = x2.shape[0]
    tm = TM_MLP
    row = pl.BlockSpec((tm, D_MODEL), lambda i: (i, 0))
    consts = (g3, wu, wd, g4)
    return pl.pallas_call(
        _mlp_kernel,
        grid=(n // tm,),
        in_specs=[row] + [_const_spec(c.shape) for c in consts],
        out_specs=row,
        out_shape=jax.ShapeDtypeStruct((n, D_MODEL), F32),
        compiler_params=pltpu.CompilerParams(
            dimension_semantics=("arbitrary",), vmem_limit_bytes=VMEM_LIMIT),
        name="mlp",
    )(x2, *consts)


def _rope_constants():
    half_a = MOBA_ROT_DIM // 2
    half_b = MLA_ROPE // 2
    inv_a = ROPE_THETA ** (-jnp.arange(half_a, dtype=F32) * (2.0 / MOBA_ROT_DIM))
    inv_b = ROPE_THETA ** (-jnp.arange(half_b, dtype=F32) * (2.0 / MLA_ROPE))
    invf = jnp.concatenate([inv_a, inv_b, jnp.zeros((N_TRIG - half_a - half_b,), F32)])
    one_slot = half_a + half_b
    lane = jnp.arange(LANES)
    slot = jnp.arange(LANES)[:, None]
    d = lane % HEAD_DIM
    e = lane - MLA_ROPE_LANE
    in_a = d < MOBA_ROT_DIM
    in_b = (e >= 0) & (e < MLA_ROPE)
    cos_slot_a = jnp.where(in_a, d % half_a, one_slot)
    cos_slot_b = jnp.where(in_b, half_a + e % half_b, one_slot)
    sin_slot_a = N_TRIG + d % half_a
    sin_slot_b = N_TRIG + half_a + e % half_b
    tables = [
        (slot == cos_slot_a) * 1.0,
        (slot == sin_slot_a) * jnp.where(d < half_a, -1.0, 0.0),
        (slot == sin_slot_a) * jnp.where((d >= half_a) & in_a, 1.0, 0.0),
        (slot == cos_slot_b) * 1.0,
        (slot == sin_slot_b) * jnp.where((e >= 0) & (e < half_b), -1.0, 0.0),
        (slot == sin_slot_b) * jnp.where((e >= half_b) & in_b, 1.0, 0.0),
    ]
    place = jnp.concatenate(tables, axis=1)
    place = jnp.concatenate([place, place], axis=0).astype(BF16)
    return invf[:, None], place


def _layer_weights(w_in, w_uq, w_ukv):
    o_kr = 3 * MOBA_WIDTH + MLA_Q_LORA + MLA_KV_LORA
    o_gate = o_kr + MLA_ROPE
    w_kr = jnp.pad(w_in[:, o_kr:o_gate],
                   ((0, 0), (MLA_ROPE_LANE, LANES - MLA_ROPE_LANE - MLA_ROPE)))
    sa = HEAD_DIM ** -0.5 * LOG2E
    sb = (HEAD_DIM + MLA_ROPE) ** -0.5 * LOG2E
    w1 = jnp.concatenate([w_in[:, :MOBA_WIDTH] * sa, w_in[:, MOBA_WIDTH:o_kr], w_kr],
                         axis=1).astype(BF16)
    wg = w_in[:, o_gate:].astype(BF16)
    pad_head = lambda w: jnp.pad(w, ((0, 0), (0, 0), (0, LANES - w.shape[-1]))).reshape(
        w.shape[0], HEADS * LANES)
    wuq = pad_head(w_uq.reshape(MLA_Q_LORA, HEADS, HEAD_DIM + MLA_ROPE) * sb).astype(BF16)
    ukv = w_ukv.reshape(MLA_KV_LORA, HEADS, 2 * HEAD_DIM)
    wukv = jnp.concatenate(
        [pad_head(ukv[..., :HEAD_DIM]), ukv[..., HEAD_DIM:].reshape(MLA_KV_LORA, HEADS * HEAD_DIM)],
        axis=1).astype(BF16)
    return w1, wg, wuq, wukv


def kernel(x, positions, g_pre_mix, w_in, b_gate, g_q_norm, w_uq, g_kv_norm, w_ukv,
           w_branch_a, w_branch_b, w_out, g_post_mix, g_pre_mlp, w_up, w_down, g_post_mlp):
    b, s, d = x.shape
    n = b * s
    assert d == D_MODEL and s % KV_TILE == 0 and n % TM_PROJ == 0
    invf, place = _rope_constants()
    pos3 = positions.reshape(n // TM_PROJ, 1, TM_PROJ)
    x2 = x.reshape(n, d)
    row = lambda v: v[None, :]
    for l in range(w_in.shape[0]):
        w1, wg, wuq, wukv = _layer_weights(w_in[l], w_uq[l], w_ukv[l])
        qa, ka, va, qm, km, vb = _proj_call(
            pos3, x2, row(g_pre_mix[l]), w1, row(g_q_norm[l]), wuq, row(g_kv_norm[l]), wukv,
            invf, place)
        o_a = _attn_call(qa.reshape(b, s, -1), ka.reshape(b, s, -1), va.reshape(b, s, -1),
                         moba=True)
        o_b = _attn_call(qm.reshape(b, s, -1), km.reshape(b, s, -1), vb.reshape(b, s, -1),
                         moba=False)
        x2 = _mix_call(x2, o_a.reshape(n, -1), o_b.reshape(n, -1), row(g_pre_mix[l]), wg,
                       row(b_gate[l]), w_branch_a[l].astype(BF16), w_branch_b[l].astype(BF16),
                       w_out[l].astype(BF16), row(g_post_mix[l]))
        x2 = _mlp_call(x2, row(g_pre_mlp[l]), w_up[l].astype(BF16), w_down[l].astype(BF16),
                       row(g_post_mlp[l]))
    return x2.reshape(b, s, d)
```
